```python
import jax, jax.numpy as jnp
from jax import lax
import numpy as np

D_MODEL = 1024
BATCH = 4
SEQ = 4096
DEPTH = 1

RMS_EPS = 1e-6
N_MOD = 6
POOL_WINDOWS = (2, 4, 8, 16)
POOL_WIDTH = D_MODEL // 2
POOL_GROUP = POOL_WIDTH // len(POOL_WINDOWS)
MLA_HEADS = 8
QK_NOPE = D_MODEL // 16
QK_ROPE = D_MODEL // 32
V_HEAD = D_MODEL // 16
Q_LORA = D_MODEL // 4
KV_LORA = D_MODEL // 8
ROPE_THETA = 10000.0
Q_BLOCK = 128
MLA_WIDTH = MLA_HEADS * V_HEAD
IN_WIDTH = POOL_WIDTH + Q_LORA + KV_LORA + QK_ROPE + 2 * D_MODEL
PEER_HEADS = 8
PEER_KEYS = 128
PEER_EXPERTS = PEER_KEYS * PEER_KEYS
PEER_QUERY = 256
PEER_TOPK = 16
PEER_CHUNK = 128

kernel_name = "hybrid_pool_mla_peer_block"


def rmsnorm(x, gain):
    xf = x.astype(jnp.float32)
    y = xf * lax.rsqrt(jnp.mean(xf * xf, axis=-1, keepdims=True) + RMS_EPS)
    return (y * gain.astype(jnp.float32)).astype(x.dtype)


def rope_tables(seq, dtype):
    pos = jnp.arange(seq, dtype=jnp.float32)
    inv = 1.0 / (ROPE_THETA ** (jnp.arange(0, QK_ROPE, 2, dtype=jnp.float32) / QK_ROPE))
    ang = pos[:, None] * inv[None, :]
    ang = jnp.concatenate([ang, ang], axis=-1)
    return jnp.cos(ang).astype(dtype), jnp.sin(ang).astype(dtype)


def apply_rope(x, cos, sin):
    half = x.shape[-1] // 2
    rot = jnp.concatenate([-x[..., half:], x[..., :half]], axis=-1)
    return x * cos + rot * sin


def multiscale_pool(xp, w_group, scale):
    b, s, _ = xp.shape
    ng = len(POOL_WINDOWS)
    xg = xp.reshape(b, s, ng, POOL_GROUP).astype(jnp.float32)
    csp = jnp.concatenate([jnp.zeros((b, 1, ng, POOL_GROUP), jnp.float32),
                           jnp.cumsum(xg, axis=1)], axis=1)
    t = jnp.arange(s)
    pooled = []
    for g, w in enumerate(POOL_WINDOWS):
        upper = csp[:, 1:, g]
        lower = jnp.concatenate([jnp.zeros((b, w - 1, POOL_GROUP), jnp.float32),
                                 csp[:, :s - w + 1, g]], axis=1)
        count = jnp.minimum(t + 1, w).astype(jnp.float32)[None, :, None]
        pooled.append((upper - lower) / count)
    mixed = (jnp.stack(pooled, axis=2) - xg).astype(xp.dtype)
    mixed = jnp.einsum('bsgc,gcd->bsgd', mixed, w_group)
    return mixed.reshape(b, s, POOL_WIDTH) * scale


def causal_block_attention(q_nope, q_rope, k_nope, k_rope, v):
    s = q_nope.shape[1]
    sm_scale = (QK_NOPE + QK_ROPE) ** -0.5
    outs = []
    for i in range(s // Q_BLOCK):
        q0, q1 = i * Q_BLOCK, (i + 1) * Q_BLOCK
        sc = (jnp.einsum('bqhd,bkhd->bhqk', q_nope[:, q0:q1], k_nope[:, :q1])
              + jnp.einsum('bqhd,bkd->bhqk', q_rope[:, q0:q1], k_rope[:, :q1]))
        sc = sc.astype(jnp.float32) * sm_scale
        qpos = q0 + jnp.arange(Q_BLOCK)[:, None]
        kpos = jnp.arange(q1)[None, :]
        sc = jnp.where(kpos <= qpos, sc, -jnp.inf)
        p = jax.nn.softmax(sc, axis=-1).astype(v.dtype)
        outs.append(jnp.einsum('bhqk,bkhd->bqhd', p, v[:, :q1]))
    return jnp.concatenate(outs, axis=1)


def latent_attention(q_lat, kv_lat, k_rope, g_q_lat, w_q_up, g_kv_lat, w_kv_up):
    b, s, _ = q_lat.shape
    cos, sin = rope_tables(s, q_lat.dtype)
    q = (rmsnorm(q_lat, g_q_lat) @ w_q_up).reshape(b, s, MLA_HEADS, QK_NOPE + QK_ROPE)
    kv = (rmsnorm(kv_lat, g_kv_lat) @ w_kv_up).reshape(b, s, MLA_HEADS, QK_NOPE + V_HEAD)
    q_nope = q[..., :QK_NOPE]
    q_rope = apply_rope(q[..., QK_NOPE:], cos[:, None, :], sin[:, None, :])
    k_nope, v = kv[..., :QK_NOPE], kv[..., QK_NOPE:]
    k_rope = apply_rope(k_rope, cos, sin)
    o = causal_block_attention(q_nope, q_rope, k_nope, k_rope, v)
    return o.reshape(b, s, MLA_WIDTH)


def peer(xn, w_query, sub_keys1, sub_keys2, expert_u, expert_v):
    b, s, d = xn.shape
    half = PEER_QUERY // 2
    q = (xn @ w_query).reshape(b, s, PEER_HEADS, PEER_QUERY)
    s1 = jnp.einsum('bshd,hkd->bshk', q[..., :half], sub_keys1)
    s2 = jnp.einsum('bshd,hkd->bshk', q[..., half:], sub_keys2)
    v1, i1 = lax.top_k(s1, PEER_TOPK)
    v2, i2 = lax.top_k(s2, PEER_TOPK)
    cand = (v1[..., :, None] + v2[..., None, :]).reshape(b, s, PEER_HEADS, PEER_TOPK * PEER_TOPK)
    sc, ci = lax.top_k(cand, PEER_TOPK)
    eidx = (jnp.take_along_axis(i1, ci // PEER_TOPK, axis=-1) * PEER_KEYS
            + jnp.take_along_axis(i2, ci % PEER_TOPK, axis=-1))
    gate = jax.nn.softmax(sc.astype(jnp.float32), axis=-1).astype(xn.dtype)
    n_chunks = (b * s) // PEER_CHUNK
    xs = xn.reshape(n_chunks, PEER_CHUNK, d)
    es = eidx.reshape(n_chunks, PEER_CHUNK, PEER_HEADS, PEER_TOPK)
    gs = gate.reshape(n_chunks, PEER_CHUNK, PEER_HEADS, PEER_TOPK)

    def chunk(args):
        xc, ec, gc = args
        act = jnp.einsum('cd,chkd->chk', xc, expert_u[ec])
        w = gc * jax.nn.gelu(act)
        return jnp.einsum('chk,chkd->cd', w, expert_v[ec])

    y = lax.map(chunk, (xs, es, gs))
    return y.reshape(b, s, d)


def setup_inputs(seed: int = 0) -> dict:
    key = jax.random.key(seed)
    ks = jax.random.split(key, 24)
    L, D = DEPTH, D_MODEL

    def nrm(k, shape, scale):
        return jax.random.normal(k, shape, jnp.float32) * scale

    def gain(k, shape):
        return 1.0 + nrm(k, shape, 0.01)

    return {
        "x": nrm(ks[0], (BATCH, SEQ, D), 1.0),
        "c": nrm(ks[1], (BATCH, D), 1.0),
        "w_ada": nrm(ks[2], (L, D, N_MOD * D), 0.5 * D ** -0.5),
        "b_ada": nrm(ks[3], (L, N_MOD * D), 0.01),
        "g_pre1": gain(ks[4], (L, D)),
        "g_post1": gain(ks[5], (L, D)),
        "w_in": nrm(ks[6], (L, D, IN_WIDTH), D ** -0.5),
        "w_pool_group": nrm(ks[7], (L, len(POOL_WINDOWS), POOL_GROUP, POOL_GROUP), POOL_GROUP ** -0.5),
        "pool_scale": 1.0 + nrm(ks[8], (L, POOL_WIDTH), 0.1),
        "w_pool_o": nrm(ks[9], (L, POOL_WIDTH, D), POOL_WIDTH ** -0.5),
        "g_q_lat": gain(ks[10], (L, Q_LORA)),
        "w_q_up": nrm(ks[11], (L, Q_LORA, MLA_HEADS * (QK_NOPE + QK_ROPE)), Q_LORA ** -0.5),
        "g_kv_lat": gain(ks[12], (L, KV_LORA)),
        "w_kv_up": nrm(ks[13], (L, KV_LORA, MLA_HEADS * (QK_NOPE + V_HEAD)), KV_LORA ** -0.5),
        "w_mla_o": nrm(ks[14], (L, MLA_WIDTH, D), MLA_WIDTH ** -0.5),
        "w_out": nrm(ks[15], (L, D, D), D ** -0.5),
        "g_pre2": gain(ks[16], (L, D)),
        "g_post2": gain(ks[17], (L, D)),
        "w_query": nrm(ks[18], (L, D, PEER_HEADS * PEER_QUERY), D ** -0.5),
        "sub_keys1": nrm(ks[19], (L, PEER_HEADS, PEER_KEYS, PEER_QUERY // 2), (PEER_QUERY // 2) ** -0.5),
        "sub_keys2": nrm(ks[20], (L, PEER_HEADS, PEER_KEYS, PEER_QUERY // 2), (PEER_QUERY // 2) ** -0.5),
        "expert_u": nrm(ks[21], (L, PEER_EXPERTS, D), D ** -0.5),
        "expert_v": nrm(ks[22], (L, PEER_EXPERTS, D), 0.1),
    }


def reference(x, c, w_ada, b_ada, g_pre1, g_post1, w_in, w_pool_group, pool_scale, w_pool_o,
              g_q_lat, w_q_up, g_kv_lat, w_kv_up, w_mla_o, w_out, g_pre2, g_post2,
              w_query, sub_keys1, sub_keys2, expert_u, expert_v):
    h = x
    split_at = np.cumsum([POOL_WIDTH, Q_LORA, KV_LORA, QK_ROPE, D_MODEL]).tolist()
    for l in range(DEPTH):
        mod = jax.nn.silu(c) @ w_ada[l] + b_ada[l]
        shift1, scale1, gate1, shift2, scale2, gate2 = jnp.split(mod[:, None, :], N_MOD, axis=-1)

        xn = rmsnorm(h, g_pre1[l]) * (1.0 + scale1) + shift1
        proj = xn @ w_in[l]
        x_pool, q_lat, kv_lat, k_rope, gate_a, gate_b = jnp.split(proj, split_at, axis=-1)
        y_pool = multiscale_pool(x_pool, w_pool_group[l], pool_scale[l]) @ w_pool_o[l]
        y_mla = latent_attention(q_lat, kv_lat, k_rope, g_q_lat[l], w_q_up[l],
                                 g_kv_lat[l], w_kv_up[l]) @ w_mla_o[l]
        merged = jax.nn.sigmoid(gate_a) * y_pool + jax.nn.sigmoid(gate_b) * y_mla
        h = h + gate1 * rmsnorm(merged @ w_out[l], g_post1[l])

        xn2 = rmsnorm(h, g_pre2[l]) * (1.0 + scale2) + shift2
        y2 = peer(xn2, w_query[l], sub_keys1[l], sub_keys2[l], expert_u[l], expert_v[l])
        h = h + gate2 * rmsnorm(y2, g_post2[l])
    return h
```

```python
import functools

import jax
import jax.numpy as jnp
from jax import lax
from jax.experimental import pallas as pl
from jax.experimental.pallas import tpu as pltpu

F32 = jnp.float32
BF16 = jnp.bfloat16

RMS_EPS = 1e-6
POOL_WINDOWS = (2, 4, 8, 16)
POOL_HALO = 16
LANES = 128
N_HEADS = 8
QK_NOPE = 64
QK_ROPE = 32
V_HEAD = 64
HEAD_PAD = 128
ROPE_THETA = 10000.0
PEER_HEADS = 8
PEER_KEYS = 128
PEER_TOPK = 16
NEG_INF = float("-inf")
VMEM_LIMIT = 56 * 1024 * 1024

TM_IN = 512
TQ = 512
TKV = 512
TR = 256
TM_PEER = 512
EC = 1024


def _rms(x, gain):
    return x * lax.rsqrt(jnp.mean(x * x, axis=-1, keepdims=True) + RMS_EPS) * gain


def _dot(a, b):
    return jnp.dot(a, b, preferred_element_type=F32)


def _ada_kernel(c_ref, w_ref, b_ref, o_ref):
    c = c_ref[...]
    sc = c * jax.nn.sigmoid(c)
    o_ref[...] = jnp.dot(sc, w_ref[...], preferred_element_type=F32,
                         precision=lax.Precision.HIGHEST) + b_ref[...]


def _ada(c8, w, b):
    d, n = w.shape
    tn = 1536
    return pl.pallas_call(
        _ada_kernel,
        grid=(n // tn,),
        in_specs=[pl.BlockSpec((8, d), lambda j: (0, 0)),
                  pl.BlockSpec((d, tn), lambda j: (0, j)),
                  pl.BlockSpec((1, tn), lambda j: (0, j))],
        out_specs=pl.BlockSpec((8, tn), lambda j: (0, j)),
        out_shape=jax.ShapeDtypeStruct((8, n), F32),
        compiler_params=pltpu.CompilerParams(dimension_semantics=("arbitrary",),
                                             vmem_limit_bytes=VMEM_LIMIT),
        name="ada",
    )(c8, w, b)


def _inproj_kernel(x_ref, sc_ref, sh_ref, gpre_ref, wa_ref, wg_ref, wgrp_ref, pscale_ref, wpo_ref,
                   gq_ref, wq_ref, wqr_ref, gkv_ref, wk_ref, wv_ref, tab_ref,
                   q_ref, k_ref, v_ref, p_ref, sgb_ref, hal_ref, *, tiles_per_seq):
    tm = x_ref.shape[0]
    i = pl.program_id(0)
    ti = i % tiles_per_seq
    xn = _rms(x_ref[...], gpre_ref[...]) * (1.0 + sc_ref[0]) + sh_ref[0]
    xb = xn.astype(BF16)
    pa = _dot(xb, wa_ref[...])
    xp = pa[:, 0:512]
    ql = pa[:, 512:768]
    kvl = pa[:, 768:896]
    kr = pa[:, 896:1024]
    krr = pa[:, 1024:1152]

    @pl.when(ti == 0)
    def _():
        hal_ref[0:POOL_HALO, :] = jnp.zeros((POOL_HALO, 512), F32)

    hal_ref[POOL_HALO:POOL_HALO + tm, :] = xp
    pos = lax.broadcasted_iota(jnp.int32, (tm, LANES), 0) + ti * tm
    outs = []
    for g, w in enumerate(POOL_WINDOWS):
        lo, hi = g * LANES, (g + 1) * LANES
        ssum = hal_ref[POOL_HALO:POOL_HALO + tm, lo:hi]
        for kk in range(1, w):
            ssum = ssum + hal_ref[POOL_HALO - kk:POOL_HALO - kk + tm, lo:hi]
        cnt = jnp.minimum(pos + 1, w).astype(F32)
        mixed = ssum / cnt - xp[:, lo:hi]
        outs.append(_dot(mixed.astype(BF16), wgrp_ref[g]) * pscale_ref[:, lo:hi])
    hal_ref[0:POOL_HALO, :] = hal_ref[tm:tm + POOL_HALO, :]
    y_pool = _dot(jnp.concatenate(outs, axis=1).astype(BF16), wpo_ref[...])

    pg = _dot(xb, wg_ref[...])
    p_ref[...] = jax.nn.sigmoid(pg[:, :1024]) * y_pool
    sgb_ref[...] = jax.nn.sigmoid(pg[:, 1024:])

    tab = tab_ref[...]
    cos_k, sin_k = tab[:, 0:128], tab[:, 128:256]
    cos_q, sin_q = tab[:, 256:384], tab[:, 384:512]
    qb = _rms(ql, gq_ref[...]).astype(BF16)
    q = _dot(qb, wq_ref[...])
    qr = _dot(qb, wqr_ref[...])
    q_ref[...] = (q * jnp.tile(cos_q, (1, N_HEADS)) + qr * jnp.tile(sin_q, (1, N_HEADS))).astype(BF16)
    kvb = _rms(kvl, gkv_ref[...]).astype(BF16)
    k_rope = kr * cos_k + krr * sin_k
    k_ref[...] = (_dot(kvb, wk_ref[...]) + jnp.tile(k_rope, (1, N_HEADS))).astype(BF16)
    v_ref[...] = _dot(kvb, wv_ref[...]).astype(BF16)


def _inproj(x2, scale1, shift1, g_pre1, wa, wg, wgrp, pscale, wpo, gq, wq, wqr, gkv, wk, wv, tab, seq):
    t, d = x2.shape
    tm = TM_IN
    tps = seq // tm
    full = lambda a: pl.BlockSpec(a.shape, lambda i: (0,) * a.ndim)
    per_batch = pl.BlockSpec((1, 1, d), lambda i: (i // tps, 0, 0))
    tok = lambda n: pl.BlockSpec((tm, n), lambda i: (i, 0))
    return pl.pallas_call(
        functools.partial(_inproj_kernel, tiles_per_seq=tps),
        grid=(t // tm,),
        in_specs=[tok(d), per_batch, per_batch, full(g_pre1), full(wa), full(wg), full(wgrp), full(pscale),
                  full(wpo), full(gq), full(wq), full(wqr), full(gkv), full(wk), full(wv),
                  pl.BlockSpec((tm, 512), lambda i: (i % tps, 0))],
        out_specs=[tok(1024), tok(1024), tok(512), tok(1024), tok(1024)],
        out_shape=[jax.ShapeDtypeStruct((t, 1024), BF16), jax.ShapeDtypeStruct((t, 1024), BF16),
                   jax.ShapeDtypeStruct((t, 512), BF16), jax.ShapeDtypeStruct((t, 1024), F32),
                   jax.ShapeDtypeStruct((t, 1024), F32)],
        scratch_shapes=[pltpu.VMEM((POOL_HALO + tm, 512), F32)],
        compiler_params=pltpu.CompilerParams(dimension_semantics=("arbitrary",),
                                             vmem_limit_bytes=VMEM_LIMIT),
        name="inproj",
    )(x2, scale1, shift1, g_pre1, wa, wg, wgrp, pscale, wpo, gq, wq, wqr, gkv, wk, wv, tab)


def _attn_kernel(q_ref, k_ref, v_ref, o_ref, m_ref, l_ref, acc_ref):
    tq, tkv = q_ref.shape[0], k_ref.shape[0]
    qi = pl.program_id(2)
    ki = pl.program_id(3)

    @pl.when(ki == 0)
    def _():
        m_ref[...] = jnp.full(m_ref.shape, NEG_INF, F32)
        l_ref[...] = jnp.zeros(l_ref.shape, F32)
        acc_ref[...] = jnp.zeros(acc_ref.shape, F32)

    @pl.when(ki <= qi)
    def _():
        qpos = lax.broadcasted_iota(jnp.int32, (tq, tkv), 0) + qi * tq
        kpos = lax.broadcasted_iota(jnp.int32, (tq, tkv), 1) + ki * tkv
        allowed = kpos <= qpos
        vp = v_ref[...]
        for hh in range(2):
            qh = q_ref[:, hh * HEAD_PAD:(hh + 1) * HEAD_PAD]
            kh = k_ref[:, hh * HEAD_PAD:(hh + 1) * HEAD_PAD]
            s = lax.dot_general(qh, kh, (((1,), (1,)), ((), ())), preferred_element_type=F32)
            s = jnp.where(allowed, s, NEG_INF)
            m_prev = m_ref[hh]
            m_new = jnp.maximum(m_prev, jnp.max(s, axis=1, keepdims=True))
            alpha = jnp.exp(m_prev - m_new)
            p = jnp.exp(s - pltpu.repeat(m_new, tkv // LANES, axis=1))
            l_ref[hh] = alpha * l_ref[hh] + jnp.sum(p, axis=1, keepdims=True)
            acc_ref[hh] = alpha * acc_ref[hh] + _dot(p.astype(BF16), vp)
            m_ref[hh] = m_new

    @pl.when(ki == qi)
    def _():
        lane = lax.broadcasted_iota(jnp.int32, (tq, LANES), 1)
        o0 = acc_ref[0] / l_ref[0]
        o1 = acc_ref[1] / l_ref[1]
        o_ref[...] = jnp.where(lane < V_HEAD, o0, o1).astype(BF16)


def _attention(q, k, v, batch, seq):
    t = q.shape[0]
    nq, nkv = seq // TQ, seq // TKV
    return pl.pallas_call(
        _attn_kernel,
        grid=(batch, N_HEADS // 2, nq, nkv),
        in_specs=[pl.BlockSpec((TQ, 2 * HEAD_PAD), lambda b, hp, qi, ki: (b * nq + qi, hp)),
                  pl.BlockSpec((TKV, 2 * HEAD_PAD), lambda b, hp, qi, ki: (b * nkv + jnp.minimum(ki, qi), hp)),
                  pl.BlockSpec((TKV, 2 * V_HEAD), lambda b, hp, qi, ki: (b * nkv + jnp.minimum(ki, qi), hp))],
        out_specs=pl.BlockSpec((TQ, 2 * V_HEAD), lambda b, hp, qi, ki: (b * nq + qi, hp)),
        out_shape=jax.ShapeDtypeStruct((t, N_HEADS * V_HEAD), BF16),
        scratch_shapes=[pltpu.VMEM((2, TQ, LANES), F32), pltpu.VMEM((2, TQ, LANES), F32),
                        pltpu.VMEM((2, TQ, LANES), F32)],
        compiler_params=pltpu.CompilerParams(
            dimension_semantics=("arbitrary", "arbitrary", "arbitrary", "arbitrary"),
            vmem_limit_bytes=VMEM_LIMIT),
        name="attn",
    )(q, k, v)


def _post_kernel(o_ref, p_ref, sgb_ref, x_ref, wmo_ref, wout_ref, gpost_ref, gate1_ref, gpre2_ref,
                 sc2_ref, sh2_ref, h1_ref, xt_ref):
    y_mla = _dot(o_ref[...], wmo_ref[...])
    merged = p_ref[...] + sgb_ref[...] * y_mla
    z = _dot(merged.astype(BF16), wout_ref[...])
    h1 = x_ref[...] + gate1_ref[0] * _rms(z, gpost_ref[...])
    h1_ref[...] = h1
    xn2 = _rms(h1, gpre2_ref[...]) * (1.0 + sc2_ref[0]) + sh2_ref[0]
    xt_ref[...] = xn2.T.astype(BF16)


def _post(o, p, sgb, x2, wmo, wout, g_post1, gate1, g_pre2, scale2, shift2, seq):
    t, d = x2.shape
    tm = TM_IN
    tps = seq // tm
    full = lambda a: pl.BlockSpec(a.shape, lambda i: (0,) * a.ndim)
    per_batch = pl.BlockSpec((1, 1, d), lambda i: (i // tps, 0, 0))
    tok = lambda n: pl.BlockSpec((tm, n), lambda i: (i, 0))
    return pl.pallas_call(
        _post_kernel,
        grid=(t // tm,),
        in_specs=[tok(512), tok(d), tok(d), tok(d), full(wmo), full(wout), full(g_post1), per_batch,
                  full(g_pre2), per_batch, per_batch],
        out_specs=[tok(d), pl.BlockSpec((d, tm), lambda i: (0, i))],
        out_shape=[jax.ShapeDtypeStruct((t, d), F32), jax.ShapeDtypeStruct((d, t), BF16)],
        compiler_params=pltpu.CompilerParams(dimension_semantics=("arbitrary",),
                                             vmem_limit_bytes=VMEM_LIMIT),
        name="post",
    )(o, p, sgb, x2, wmo, wout, g_post1, gate1, g_pre2, scale2, shift2)


def _top16(s, key_idx):
    cur = s
    rank = jnp.full(s.shape, float(PEER_TOPK), F32)
    vals = []
    for a in range(PEER_TOPK):
        m = jnp.max(cur, axis=0, keepdims=True)
        first = jnp.min(jnp.where(cur == m, key_idx, float(PEER_KEYS)), axis=0, keepdims=True)
        hit = key_idx == first
        rank = jnp.where(hit, float(a), rank)
        cur = jnp.where(hit, NEG_INF, cur)
        vals.append(m)
    return vals, rank


def _route_kernel(xt_ref, wq_ref, k1_ref, k2_ref, a_ref, n_ref, b_ref, rb_ref, q_scr):
    tr = xt_ref.shape[1]
    q_scr[...] = _dot(wq_ref[...], xt_ref[...]).astype(BF16)
    key_idx = lax.broadcasted_iota(jnp.int32, (PEER_KEYS, tr), 0).astype(F32)
    i16 = lax.broadcasted_iota(jnp.int32, (16, tr), 0).astype(F32)
    i8 = lax.broadcasted_iota(jnp.int32, (8, tr), 0).astype(F32)
    flat = jnp.concatenate([i16] + [i8 + 16.0 * a for a in range(1, PEER_TOPK)], axis=0)

    def head(h, carry):
        base = pl.multiple_of(h * 256, 256)
        s1 = _dot(k1_ref[h], q_scr[pl.ds(base, 128), :])
        s2 = _dot(k2_ref[h], q_scr[pl.ds(base + 128, 128), :])
        v1, r1 = _top16(s1, key_idx)
        v2l, r2 = _top16(s2, key_idx)
        v2 = jnp.concatenate(v2l, axis=0)
        cand = jnp.concatenate([v1[0] + v2] + [v1[a] + v2[0:8] for a in range(1, PEER_TOPK)], axis=0)
        sel = jnp.zeros(cand.shape, F32)
        for _ in range(PEER_TOPK):
            m = jnp.max(cand, axis=0, keepdims=True)
            first = jnp.min(jnp.where(cand == m, flat, 1e9), axis=0, keepdims=True)
            hit = flat == first
            sel = jnp.where(hit, 1.0, sel)
            cand = jnp.where(hit, NEG_INF, cand)
        e2 = jnp.exp(v2 - v2l[0])
        e1 = [jnp.exp(v1[a] - v1[0]) for a in range(PEER_TOPK)]
        ee = jnp.concatenate([e1[0] * e2] + [e1[a] * e2[0:8] for a in range(1, PEER_TOPK)], axis=0)
        inv_z = 1.0 / jnp.sum(sel * ee, axis=0, keepdims=True)
        n_of_rank = [jnp.sum(sel[0:16], axis=0, keepdims=True)]
        for a in range(1, PEER_TOPK):
            n_of_rank.append(jnp.sum(sel[8 + 8 * a:16 + 8 * a], axis=0, keepdims=True))
        n_dense = jnp.zeros(s1.shape, F32)
        for a in range(PEER_TOPK):
            n_dense = jnp.where(r1 == float(a), n_of_rank[a], n_dense)
        a_ref[h] = jnp.where(r1 < float(PEER_TOPK), jnp.exp(s1 - v1[0]), 0.0) * inv_z
        n_ref[h] = n_dense
        b_ref[h] = jnp.where(r2 < float(PEER_TOPK), jnp.exp(s2 - v2l[0]), 0.0)
        rb_ref[h] = r2
        return carry

    lax.fori_loop(0, PEER_HEADS, head, 0)


def _route(xt, wqt, k1, k2):
    d, t = xt.shape
    full = lambda a: pl.BlockSpec(a.shape, lambda i: (0,) * a.ndim)
    out_spec = pl.BlockSpec((PEER_HEADS, PEER_KEYS, TR), lambda i: (0, 0, i))
    out_shape = jax.ShapeDtypeStruct((PEER_HEADS, PEER_KEYS, t), F32)
    return pl.pallas_call(
        _route_kernel,
        grid=(t // TR,),
        in_specs=[pl.BlockSpec((d, TR), lambda i: (0, i)), full(wqt), full(k1), full(k2)],
        out_specs=[out_spec] * 4,
        out_shape=[out_shape] * 4,
        scratch_shapes=[pltpu.VMEM((wqt.shape[0], TR), BF16)],
        compiler_params=pltpu.CompilerParams(dimension_semantics=("arbitrary",),
                                             vmem_limit_bytes=VMEM_LIMIT),
        name="route",
    )(xt, wqt, k1, k2)


def _gelu_tanh(x):
    return 0.5 * x * (1.0 + jnp.tanh(0.7978845608028654 * (x + 0.044715 * (x * x * x))))


def _peer_kernel(xt_ref, u_ref, vt_ref, a_ref, n_ref, b_ref, rb_ref, h1_ref, g2_ref, gpost_ref,
                 o_ref, acc_ref, act_ref, w_ref):
    c = pl.program_id(1)
    tm = xt_ref.shape[1]

    @pl.when(c == 0)
    def _():
        acc_ref[...] = jnp.zeros(acc_ref.shape, F32)

    act_ref[...] = _dot(u_ref[...], xt_ref[...])

    def first_key_row(il, carry):
        r0 = pl.multiple_of(il * PEER_KEYS, PEER_KEYS)
        g = jnp.zeros((PEER_KEYS, tm), F32)
        for h in range(PEER_HEADS):
            n_row = n_ref[h, pl.ds(il, 1), :]
            a_row = a_ref[h, pl.ds(il, 1), :]
            g = g + a_row * jnp.where(rb_ref[h] < n_row, b_ref[h], 0.0)
        w_ref[pl.ds(r0, PEER_KEYS), :] = (g * _gelu_tanh(act_ref[pl.ds(r0, PEER_KEYS), :])).astype(BF16)
        return carry

    lax.fori_loop(0, EC // PEER_KEYS, first_key_row, 0)
    acc_ref[...] += _dot(vt_ref[...], w_ref[...])

    @pl.when(c == pl.num_programs(1) - 1)
    def _():
        y = acc_ref[...]
        yn = y * lax.rsqrt(jnp.mean(y * y, axis=0, keepdims=True) + RMS_EPS)
        o_ref[...] = h1_ref[...] + g2_ref[0] * (yn.T * gpost_ref[...])


def _peer(xt, u, vt, a, n, b, rb, h1, gate2, g_post2, seq):
    d, t = xt.shape
    n_exp = u.shape[0]
    tm = TM_PEER
    tps = seq // tm
    rows = EC // PEER_KEYS
    return pl.pallas_call(
        _peer_kernel,
        grid=(t // tm, n_exp // EC),
        in_specs=[pl.BlockSpec((d, tm), lambda i, c: (0, i)),
                  pl.BlockSpec((EC, d), lambda i, c: (c, 0)),
                  pl.BlockSpec((d, EC), lambda i, c: (0, c)),
                  pl.BlockSpec((PEER_HEADS, rows, tm), lambda i, c: (0, c, i)),
                  pl.BlockSpec((PEER_HEADS, rows, tm), lambda i, c: (0, c, i)),
                  pl.BlockSpec((PEER_HEADS, PEER_KEYS, tm), lambda i, c: (0, 0, i)),
                  pl.BlockSpec((PEER_HEADS, PEER_KEYS, tm), lambda i, c: (0, 0, i)),
                  pl.BlockSpec((tm, d), lambda i, c: (i, 0)),
                  pl.BlockSpec((1, 1, d), lambda i, c: (i // tps, 0, 0)),
                  pl.BlockSpec((1, d), lambda i, c: (0, 0))],
        out_specs=pl.BlockSpec((tm, d), lambda i, c: (i, 0)),
        out_shape=jax.ShapeDtypeStruct((t, d), F32),
        scratch_shapes=[pltpu.VMEM((d, tm), F32), pltpu.VMEM((EC, tm), F32), pltpu.VMEM((EC, tm), BF16)],
        compiler_params=pltpu.CompilerParams(dimension_semantics=("arbitrary", "arbitrary"),
                                             vmem_limit_bytes=VMEM_LIMIT),
        name="peer",
    )(xt, u, vt, a, n, b, rb, h1, gate2, g_post2)


def _rope_tables(seq):
    pos = jnp.arange(seq, dtype=F32)
    inv = 1.0 / (ROPE_THETA ** (jnp.arange(0, QK_ROPE, 2, dtype=F32) / QK_ROPE))
    ang = pos[:, None] * inv[None, :]
    ang = jnp.concatenate([ang, ang], axis=-1)
    cos, sin = jnp.cos(ang), jnp.sin(ang)
    pad = HEAD_PAD - QK_NOPE - QK_ROPE
    cos_t = jnp.concatenate([jnp.ones((seq, QK_NOPE), F32), cos, jnp.zeros((seq, pad), F32)], axis=1)
    sin_t = jnp.concatenate([jnp.zeros((seq, QK_NOPE), F32), sin, jnp.zeros((seq, pad), F32)], axis=1)
    sm_scale = (QK_NOPE + QK_ROPE) ** -0.5
    return jnp.concatenate([cos_t, sin_t, cos_t * sm_scale, sin_t * sm_scale], axis=1)


def _rotate_half_cols(w):
    half = w.shape[-1] // 2
    return jnp.concatenate([-w[..., half:], w[..., :half]], axis=-1)


def _place(w, lo, width):
    pads = [(0, 0)] * (w.ndim - 1) + [(lo, width - lo - w.shape[-1])]
    return jnp.pad(w, pads)


def kernel(x, c, w_ada, b_ada, g_pre1, g_post1, w_in, w_pool_group, pool_scale, w_pool_o, g_q_lat, w_q_up,
           g_kv_lat, w_kv_up, w_mla_o, w_out, g_pre2, g_post2, w_query, sub_keys1, sub_keys2, expert_u,
           expert_v):
    batch, seq, d = x.shape
    assert w_ada.shape[0] == 1, "one layer"
    assert seq % TM_IN == 0 and seq % TQ == 0 and seq % TM_PEER == 0 and (batch * seq) % TR == 0
    t = batch * seq
    x2 = x.reshape(t, d)

    c8 = jnp.pad(c, ((0, 8 - batch), (0, 0)))
    mod = _ada(c8, w_ada[0], b_ada[0][None, :])[:batch]
    shift1, scale1, gate1, shift2, scale2, gate2 = [m[:, None, :] for m in jnp.split(mod, 6, axis=-1)]

    wi = w_in[0]
    w_kr = wi[:, 896:928]
    wa = jnp.concatenate([wi[:, :896], _place(w_kr, QK_NOPE, HEAD_PAD),
                          _place(_rotate_half_cols(w_kr), QK_NOPE, HEAD_PAD)], axis=1).astype(BF16)
    wg = wi[:, 928:].astype(BF16)
    wq3 = w_q_up[0].reshape(-1, N_HEADS, QK_NOPE + QK_ROPE)
    wq = _place(wq3, 0, HEAD_PAD).reshape(-1, N_HEADS * HEAD_PAD).astype(BF16)
    wqr = _place(_rotate_half_cols(wq3[..., QK_NOPE:]), QK_NOPE, HEAD_PAD)
    wqr = wqr.reshape(-1, N_HEADS * HEAD_PAD).astype(BF16)
    wkv3 = w_kv_up[0].reshape(-1, N_HEADS, QK_NOPE + V_HEAD)
    wk = _place(wkv3[..., :QK_NOPE], 0, HEAD_PAD).reshape(-1, N_HEADS * HEAD_PAD).astype(BF16)
    wv = wkv3[..., QK_NOPE:].reshape(-1, N_HEADS * V_HEAD).astype(BF16)

    q, k, v, p, sgb = _inproj(
        x2, scale1, shift1, g_pre1, wa, wg, w_pool_group[0].astype(BF16), pool_scale, w_pool_o[0].astype(BF16),
        g_q_lat, wq, wqr, g_kv_lat, wk, wv, _rope_tables(seq), seq)
    o = _attention(q, k, v, batch, seq)
    h1, xt = _post(o, p, sgb, x2, w_mla_o[0].astype(BF16), w_out[0].astype(BF16), g_post1, gate1, g_pre2,
                   scale2, shift2, seq)
    a, n, b, rb = _route(xt, w_query[0].T.astype(BF16), sub_keys1[0].astype(BF16), sub_keys2[0].astype(BF16))
    out = _peer(xt, expert_u[0].astype(BF16), expert_v[0].T.astype(BF16), a, n, b, rb, h1, gate2, g_post2, seq)
    return out.reshape(batch, seq, d)
```

```python
import functools

import jax
import jax.numpy as jnp
from jax import lax
from jax.experimental import pallas as pl
from jax.experimental.pallas import tpu as pltpu

F32 = jnp.float32
BF16 = jnp.bfloat16

RMS_EPS = 1e-6
POOL_WINDOWS = (2, 4, 8, 16)
POOL_HALO = 16
LANES = 128
BF16_ROWS = 16
N_HEADS = 8
QK_NOPE = 64
QK_ROPE = 32
V_HEAD = 64
HEAD_PAD = 128
ROPE_THETA = 10000.0
PEER_HEADS = 8
PEER_KEYS = 128
PEER_TOPK = 16
NEG_INF = float("-inf")
LOG2_E = 1.4426950408889634
VMEM_LIMIT = 56 * 1024 * 1024

TM_IN = 512
TQ = 512
TR = 256
TM_PEER = 1024
EC = 1024


def _rms(x, gain):
    return x * lax.rsqrt(jnp.mean(x * x, axis=-1, keepdims=True) + RMS_EPS) * gain


def _dot(a, b):
    return jnp.dot(a, b, preferred_element_type=F32)


def _ada_kernel(c_ref, w_ref, b_ref, o_ref):
    c = c_ref[...]
    sc = c * jax.nn.sigmoid(c)
    o_ref[...] = jnp.dot(sc, w_ref[...], preferred_element_type=F32,
                         precision=lax.Precision.HIGHEST) + b_ref[...]


def _ada(c8, w, b):
    d, n = w.shape
    tn = 1536
    return pl.pallas_call(
        _ada_kernel,
        grid=(n // tn,),
        in_specs=[pl.BlockSpec((8, d), lambda j: (0, 0)),
                  pl.BlockSpec((d, tn), lambda j: (0, j)),
                  pl.BlockSpec((1, tn), lambda j: (0, j))],
        out_specs=pl.BlockSpec((8, tn), lambda j: (0, j)),
        out_shape=jax.ShapeDtypeStruct((8, n), F32),
        compiler_params=pltpu.CompilerParams(dimension_semantics=("arbitrary",),
                                             vmem_limit_bytes=VMEM_LIMIT),
        name="ada",
    )(c8, w, b)


def _inproj_kernel(x_ref, sc_ref, sh_ref, gpre_ref, wa_ref, wg_ref, wgrp_ref, pscale_ref, wpo_ref,
                   gq_ref, wq_ref, wqr_ref, gkv_ref, wk_ref, wv_ref, tab_ref,
                   q_ref, k_ref, v_ref, p_ref, sgb_ref, hal_ref, *, tiles_per_seq):
    tm = x_ref.shape[0]
    i = pl.program_id(0)
    ti = i % tiles_per_seq
    xn = _rms(x_ref[...], gpre_ref[...]) * (1.0 + sc_ref[0]) + sh_ref[0]
    xb = xn.astype(BF16)
    pa = _dot(xb, wa_ref[...])
    xp = pa[:, 0:512]
    ql = pa[:, 512:768]
    kvl = pa[:, 768:896]
    kr = pa[:, 896:1024]
    krr = pa[:, 1024:1152]

    @pl.when(ti == 0)
    def _():
        hal_ref[0:POOL_HALO, :] = jnp.zeros((POOL_HALO, 512), F32)

    hal_ref[POOL_HALO:POOL_HALO + tm, :] = xp
    pos = lax.broadcasted_iota(jnp.int32, (tm, LANES), 0) + ti * tm
    outs = []
    for g, w in enumerate(POOL_WINDOWS):
        lo, hi = g * LANES, (g + 1) * LANES
        ssum = hal_ref[POOL_HALO:POOL_HALO + tm, lo:hi]
        for kk in range(1, w):
            ssum = ssum + hal_ref[POOL_HALO - kk:POOL_HALO - kk + tm, lo:hi]
        cnt = jnp.minimum(pos + 1, w).astype(F32)
        mixed = ssum / cnt - xp[:, lo:hi]
        outs.append(_dot(mixed.astype(BF16), wgrp_ref[g]) * pscale_ref[:, lo:hi])
    hal_ref[0:POOL_HALO, :] = hal_ref[tm:tm + POOL_HALO, :]
    y_pool = _dot(jnp.concatenate(outs, axis=1).astype(BF16), wpo_ref[...])

    pg = _dot(xb, wg_ref[...])
    p_ref[...] = jax.nn.sigmoid(pg[:, :1024]) * y_pool
    sgb_ref[...] = jax.nn.sigmoid(pg[:, 1024:])

    tab = tab_ref[...]
    cos_k, sin_k = tab[:, 0:128], tab[:, 128:256]
    cos_q, sin_q = tab[:, 256:384], tab[:, 384:512]
    qb = _rms(ql, gq_ref[...]).astype(BF16)
    q = _dot(qb, wq_ref[...])
    qr = _dot(qb, wqr_ref[...])
    q_ref[...] = (q * jnp.tile(cos_q, (1, N_HEADS)) + qr * jnp.tile(sin_q, (1, N_HEADS))).astype(BF16)
    kvb = _rms(kvl, gkv_ref[...]).astype(BF16)
    k_rope = kr * cos_k + krr * sin_k
    k_ref[...] = (_dot(kvb, wk_ref[...]) + jnp.tile(k_rope, (1, N_HEADS))).astype(BF16)
    v_ref[...] = _dot(kvb, wv_ref[...]).astype(BF16)


def _inproj(x2, scale1, shift1, g_pre1, wa, wg, wgrp, pscale, wpo, gq, wq, wqr, gkv, wk, wv, tab, seq):
    t, d = x2.shape
    tm = TM_IN
    tps = seq // tm
    full = lambda a: pl.BlockSpec(a.shape, lambda i: (0,) * a.ndim)
    per_batch = pl.BlockSpec((1, 1, d), lambda i: (i // tps, 0, 0))
    tok = lambda n: pl.BlockSpec((tm, n), lambda i: (i, 0))
    return pl.pallas_call(
        functools.partial(_inproj_kernel, tiles_per_seq=tps),
        grid=(t // tm,),
        in_specs=[tok(d), per_batch, per_batch, full(g_pre1), full(wa), full(wg), full(wgrp), full(pscale),
                  full(wpo), full(gq), full(wq), full(wqr), full(gkv), full(wk), full(wv),
                  pl.BlockSpec((tm, 512), lambda i: (i % tps, 0))],
        out_specs=[tok(1024), tok(1024), tok(512), tok(1024), tok(1024)],
        out_shape=[jax.ShapeDtypeStruct((t, 1024), BF16), jax.ShapeDtypeStruct((t, 1024), BF16),
                   jax.ShapeDtypeStruct((t, 512), BF16), jax.ShapeDtypeStruct((t, 1024), F32),
                   jax.ShapeDtypeStruct((t, 1024), F32)],
        scratch_shapes=[pltpu.VMEM((POOL_HALO + tm, 512), F32)],
        compiler_params=pltpu.CompilerParams(dimension_semantics=("arbitrary",),
                                             vmem_limit_bytes=VMEM_LIMIT),
        name="inproj",
    )(x2, scale1, shift1, g_pre1, wa, wg, wgrp, pscale, wpo, gq, wq, wqr, gkv, wk, wv, tab)


def _attn_kernel(q_ref, k_ref, v_ref, o_ref, m_ref, l_ref, acc_ref):
    tq = q_ref.shape[0]
    qi = pl.program_id(2)
    m_ref[...] = jnp.full(m_ref.shape, NEG_INF, F32)
    l_ref[...] = jnp.zeros(l_ref.shape, F32)
    acc_ref[...] = jnp.zeros(acc_ref.shape, F32)

    def key_tile(j, masked):
        k0 = pl.multiple_of(j * tq, tq)
        vp = v_ref[pl.ds(k0, tq), :]
        for hh in range(2):
            qh = q_ref[:, hh * HEAD_PAD:(hh + 1) * HEAD_PAD]
            kh = k_ref[pl.ds(k0, tq), hh * HEAD_PAD:(hh + 1) * HEAD_PAD]
            s = lax.dot_general(qh, kh, (((1,), (1,)), ((), ())), preferred_element_type=F32)
            if masked:
                row = lax.broadcasted_iota(jnp.int32, s.shape, 0)
                col = lax.broadcasted_iota(jnp.int32, s.shape, 1)
                s = jnp.where(col <= row, s, NEG_INF)
            m_prev = m_ref[hh]
            m_new = jnp.maximum(m_prev, jnp.max(s, axis=1, keepdims=True))
            alpha = jnp.exp2(m_prev - m_new)
            p = jnp.exp2(s - jnp.tile(m_new, (1, tq // LANES)))
            l_ref[hh] = alpha * l_ref[hh] + jnp.sum(p, axis=1, keepdims=True)
            acc_ref[hh] = alpha * acc_ref[hh] + _dot(p.astype(BF16), vp)
            m_ref[hh] = m_new

    def below_diagonal(j, carry):
        key_tile(j, False)
        return carry

    lax.fori_loop(0, qi, below_diagonal, 0)
    key_tile(qi, True)
    lane = lax.broadcasted_iota(jnp.int32, (tq, LANES), 1)
    o_ref[...] = jnp.where(lane < V_HEAD, acc_ref[0] / l_ref[0], acc_ref[1] / l_ref[1]).astype(BF16)


def _attention(q, k, v, batch, seq):
    t = q.shape[0]
    nq = seq // TQ
    return pl.pallas_call(
        _attn_kernel,
        grid=(batch, N_HEADS // 2, nq),
        in_specs=[pl.BlockSpec((TQ, 2 * HEAD_PAD), lambda b, hp, qi: (b * nq + qi, hp)),
                  pl.BlockSpec((seq, 2 * HEAD_PAD), lambda b, hp, qi: (b, hp)),
                  pl.BlockSpec((seq, 2 * V_HEAD), lambda b, hp, qi: (b, hp))],
        out_specs=pl.BlockSpec((TQ, 2 * V_HEAD), lambda b, hp, qi: (b * nq + qi, hp)),
        out_shape=jax.ShapeDtypeStruct((t, N_HEADS * V_HEAD), BF16),
        scratch_shapes=[pltpu.VMEM((2, TQ, LANES), F32), pltpu.VMEM((2, TQ, LANES), F32),
                        pltpu.VMEM((2, TQ, LANES), F32)],
        compiler_params=pltpu.CompilerParams(dimension_semantics=("arbitrary", "arbitrary", "arbitrary"),
                                             vmem_limit_bytes=VMEM_LIMIT),
        name="attn",
    )(q, k, v)


def _post_kernel(o_ref, p_ref, sgb_ref, x_ref, wmo_ref, wout_ref, gpost_ref, gate1_ref, gpre2_ref,
                 sc2_ref, sh2_ref, h1_ref, xt_ref):
    y_mla = _dot(o_ref[...], wmo_ref[...])
    merged = p_ref[...] + sgb_ref[...] * y_mla
    z = _dot(merged.astype(BF16), wout_ref[...])
    h1 = x_ref[...] + gate1_ref[0] * _rms(z, gpost_ref[...])
    h1_ref[...] = h1
    xn2 = _rms(h1, gpre2_ref[...]) * (1.0 + sc2_ref[0]) + sh2_ref[0]
    xt_ref[...] = xn2.T.astype(BF16)


def _post(o, p, sgb, x2, wmo, wout, g_post1, gate1, g_pre2, scale2, shift2, seq):
    t, d = x2.shape
    tm = TM_IN
    tps = seq // tm
    full = lambda a: pl.BlockSpec(a.shape, lambda i: (0,) * a.ndim)
    per_batch = pl.BlockSpec((1, 1, d), lambda i: (i // tps, 0, 0))
    tok = lambda n: pl.BlockSpec((tm, n), lambda i: (i, 0))
    return pl.pallas_call(
        _post_kernel,
        grid=(t // tm,),
        in_specs=[tok(512), tok(d), tok(d), tok(d), full(wmo), full(wout), full(g_post1), per_batch,
                  full(g_pre2), per_batch, per_batch],
        out_specs=[tok(d), pl.BlockSpec((d, tm), lambda i: (0, i))],
        out_shape=[jax.ShapeDtypeStruct((t, d), F32), jax.ShapeDtypeStruct((d, t), BF16)],
        compiler_params=pltpu.CompilerParams(dimension_semantics=("arbitrary",),
                                             vmem_limit_bytes=VMEM_LIMIT),
        name="post",
    )(o, p, sgb, x2, wmo, wout, g_post1, gate1, g_pre2, scale2, shift2)


def _top16(s, key_idx):
    cur = s
    rank = jnp.full(s.shape, float(PEER_TOPK), F32)
    vals = []
    for a in range(PEER_TOPK):
        m = jnp.max(cur, axis=0, keepdims=True)
        first = jnp.min(jnp.where(cur == m, key_idx, float(PEER_KEYS)), axis=0, keepdims=True)
        hit = key_idx == first
        rank = jnp.where(hit, float(a), rank)
        cur = jnp.where(hit, NEG_INF, cur)
        vals.append(m)
    return vals, rank


def _route_exact(s1, s2):
    tr = s1.shape[1]
    key_idx = lax.broadcasted_iota(jnp.int32, (PEER_KEYS, tr), 0).astype(F32)
    i16 = lax.broadcasted_iota(jnp.int32, (16, tr), 0).astype(F32)
    i8 = lax.broadcasted_iota(jnp.int32, (8, tr), 0).astype(F32)
    flat = jnp.concatenate([i16] + [i8 + 16.0 * a for a in range(1, PEER_TOPK)], axis=0)
    v1, r1 = _top16(s1, key_idx)
    v2l, r2 = _top16(s2, key_idx)
    v2 = jnp.concatenate(v2l, axis=0)
    cand = jnp.concatenate([v1[0] + v2] + [v1[a] + v2[0:8] for a in range(1, PEER_TOPK)], axis=0)
    sel = jnp.zeros(cand.shape, F32)
    for _ in range(PEER_TOPK):
        m = jnp.max(cand, axis=0, keepdims=True)
        first = jnp.min(jnp.where(cand == m, flat, 1e9), axis=0, keepdims=True)
        hit = flat == first
        sel = jnp.where(hit, 1.0, sel)
        cand = jnp.where(hit, NEG_INF, cand)
    e2 = jnp.exp(v2 - v2l[0])
    e1 = [jnp.exp(v1[a] - v1[0]) for a in range(PEER_TOPK)]
    ee = jnp.concatenate([e1[0] * e2] + [e1[a] * e2[0:8] for a in range(1, PEER_TOPK)], axis=0)
    inv_z = 1.0 / jnp.sum(sel * ee, axis=0, keepdims=True)
    n_of_rank = [jnp.sum(sel[0:16], axis=0, keepdims=True)]
    for a in range(1, PEER_TOPK):
        n_of_rank.append(jnp.sum(sel[8 + 8 * a:16 + 8 * a], axis=0, keepdims=True))
    n_dense = jnp.zeros(s1.shape, F32)
    for a in range(PEER_TOPK):
        n_dense = jnp.where(r1 == float(a), n_of_rank[a], n_dense)
    a_out = jnp.where(r1 < float(PEER_TOPK), jnp.exp(s1 - v1[0]), 0.0) * inv_z
    b_out = jnp.where(r2 < float(PEER_TOPK), jnp.exp(s2 - v2l[0]), 0.0)
    return a_out, n_dense, b_out, r2


_SORT16 = ((0, 13), (1, 12), (2, 15), (3, 14), (4, 8), (5, 6), (7, 11), (9, 10),
           (0, 5), (1, 7), (2, 9), (3, 4), (6, 13), (8, 14), (10, 15), (11, 12),
           (0, 1), (2, 3), (4, 5), (6, 8), (7, 9), (10, 11), (12, 13), (14, 15),
           (0, 2), (1, 3), (4, 10), (5, 11), (6, 7), (8, 9), (12, 14), (13, 15),
           (1, 2), (3, 12), (4, 6), (5, 7), (8, 10), (9, 11), (13, 14),
           (1, 4), (2, 6), (5, 8), (7, 10), (9, 13), (11, 14),
           (2, 4), (3, 6), (9, 12), (11, 13),
           (3, 5), (6, 8), (7, 9), (10, 12),
           (3, 4), (5, 6), (7, 8), (9, 10), (11, 12),
           (6, 7), (8, 9))


def _compare_exchange(x, i, j):
    x[i], x[j] = jnp.maximum(x[i], x[j]), jnp.minimum(x[i], x[j])


def _sorted_top16(s):
    x = [s[8 * k:8 * k + 8, :] for k in range(16)]
    for i, j in _SORT16:
        _compare_exchange(x, i, j)
    for shift in (4, 2, 1):
        x = [jnp.maximum(x[k], pltpu.roll(x[15 - k], shift, 0)) for k in range(16)]
        for stride in (8, 4, 2, 1):
            for k in range(16):
                if not k & stride:
                    _compare_exchange(x, k, k + stride)
    return x


def _route_fast(s1, s2):
    tr = s1.shape[1]
    sub = lax.broadcasted_iota(jnp.int32, (8, tr), 0)
    r1, r2 = _sorted_top16(s1), _sorted_top16(s2)
    v1 = [r[0:1, :] for r in r1]
    v2 = [r[0:1, :] for r in r2]

    def one_per_sublane(rows, first):
        out = rows[first + 7]
        for r in range(6, -1, -1):
            out = jnp.where(sub == r, rows[first + r], out)
        return out

    v2_lo, v2_hi, v1_hi = one_per_sublane(r2, 0), one_per_sublane(r2, 8), one_per_sublane(r1, 8)
    cand = [r1[0] + v2_lo, r1[0] + v2_hi] + [r1[a] + v2_lo for a in range(1, 8)] + [v1_hi + r2[0]]
    cur = list(cand)
    for _ in range(PEER_TOPK):
        m = functools.reduce(jnp.maximum, cur)
        m = jnp.max(m, axis=0, keepdims=True)
        cur = [jnp.where(c == m, NEG_INF, c) for c in cur]
    sel = [jnp.where(c >= m, 1.0, 0.0) for c in cand]
    top = v1[0] + v2[0]
    z = functools.reduce(jnp.add, [s_ * jnp.exp(c - top) for s_, c in zip(sel, cand)])
    inv_z = 1.0 / jnp.sum(z, axis=0, keepdims=True)
    col = lambda x: jnp.sum(x, axis=0, keepdims=True)
    n_of_rank = ([col(sel[0]) + col(sel[1])] + [col(sel[a + 1]) for a in range(1, 8)]
                 + [sel[9][a - 8:a - 7, :] for a in range(8, 16)])
    n_dense = jnp.zeros(s1.shape, F32)
    rank2 = jnp.full(s2.shape, float(PEER_TOPK), F32)
    for r in range(PEER_TOPK - 1, -1, -1):
        n_dense = jnp.where(s1 >= v1[r], n_of_rank[r], n_dense)
        rank2 = jnp.where(s2 >= v2[r], float(r), rank2)
    count1 = col(jnp.where(s1 >= v1[PEER_TOPK - 1], 1.0, 0.0))
    count2 = col(jnp.where(s2 >= v2[PEER_TOPK - 1], 1.0, 0.0))
    count_sel = functools.reduce(jnp.add, n_of_rank)
    k = float(PEER_TOPK)
    bad = jnp.where((count1 != k) | (count2 != k) | (count_sel != k), 1.0, 0.0)
    return jnp.exp(s1 - v1[0]) * inv_z, n_dense, jnp.exp(s2 - v2[0]), rank2, bad


def _route_kernel(xt_ref, wq_ref, k1_ref, k2_ref, a_ref, n_ref, b_ref, rb_ref, q_scr):
    q_scr[...] = _dot(wq_ref[...], xt_ref[...]).astype(BF16)

    def head(h, carry):
        base = pl.multiple_of(h * 256, 256)
        s1 = _dot(k1_ref[h], q_scr[pl.ds(base, 128), :])
        s2 = _dot(k2_ref[h], q_scr[pl.ds(base + 128, 128), :])

        def store(a_out, n_out, b_out, rank2):
            a_ref[h] = a_out
            n_ref[h] = n_out
            b_ref[h] = b_out.astype(BF16)
            rb_ref[h] = rank2.astype(BF16)

        *outs, bad = _route_fast(s1, s2)
        store(*outs)

        @pl.when(jnp.max(bad) > 0.0)
        def _():
            store(*_route_exact(s1, s2))

        return carry

    lax.fori_loop(0, PEER_HEADS, head, 0)


def _route(xt, wqt, k1, k2):
    d, t = xt.shape
    full = lambda a: pl.BlockSpec(a.shape, lambda i: (0,) * a.ndim)
    out_spec = pl.BlockSpec((PEER_HEADS, PEER_KEYS, TR), lambda i: (0, 0, i))
    out_shape = lambda dt: jax.ShapeDtypeStruct((PEER_HEADS, PEER_KEYS, t), dt)
    return pl.pallas_call(
        _route_kernel,
        grid=(t // TR,),
        in_specs=[pl.BlockSpec((d, TR), lambda i: (0, i)), full(wqt), full(k1), full(k2)],
        out_specs=[out_spec] * 4,
        out_shape=[out_shape(F32), out_shape(F32), out_shape(BF16), out_shape(BF16)],
        scratch_shapes=[pltpu.VMEM((wqt.shape[0], TR), BF16)],
        compiler_params=pltpu.CompilerParams(dimension_semantics=("arbitrary",),
                                             vmem_limit_bytes=VMEM_LIMIT),
        name="route",
    )(xt, wqt, k1, k2)


def _gelu_tanh(x):
    hx = 0.5 * x
    return hx + hx * jnp.tanh(x * (0.7978845608028654 + 0.035677408136300125 * (x * x)))


def _peer_kernel(xt_ref, u_ref, vt_ref, a_ref, n_ref, b_ref, rb_ref, h1_ref, g2_ref, gpost_ref,
                 o_ref, acc_ref, act_ref, w_ref):
    c = pl.program_id(1)
    tm = xt_ref.shape[1]
    reps = PEER_KEYS // BF16_ROWS

    @pl.when(c == 0)
    def _():
        acc_ref[...] = jnp.zeros(acc_ref.shape, F32)

    act_ref[...] = _dot(u_ref[...], xt_ref[...])

    for il in range(EC // PEER_KEYS):
        rows = slice(il * PEER_KEYS, (il + 1) * PEER_KEYS)
        g = None
        for h in range(PEER_HEADS):
            n_row = jnp.tile(jnp.broadcast_to(n_ref[h, il:il + 1, :], (BF16_ROWS, tm)).astype(BF16), (reps, 1))
            a_row = jnp.tile(jnp.broadcast_to(a_ref[h, il:il + 1, :], (BF16_ROWS, tm)).astype(BF16), (reps, 1))
            term = a_row * jnp.where(rb_ref[h] < n_row, b_ref[h], jnp.zeros((), BF16))
            g = term if g is None else g + term
        w_ref[rows, :] = g * _gelu_tanh(act_ref[rows, :]).astype(BF16)
    acc_ref[...] += _dot(vt_ref[...], w_ref[...])

    @pl.when(c == pl.num_programs(1) - 1)
    def _():
        y = acc_ref[...]
        yn = y * lax.rsqrt(jnp.mean(y * y, axis=0, keepdims=True) + RMS_EPS)
        o_ref[...] = h1_ref[...] + g2_ref[0] * (yn.T * gpost_ref[...])


def _peer(xt, u, vt, a, n, b, rb, h1, gate2, g_post2, seq):
    d, t = xt.shape
    n_exp = u.shape[0]
    tm = TM_PEER
    tps = seq // tm
    rows = EC // PEER_KEYS
    return pl.pallas_call(
        _peer_kernel,
        grid=(t // tm, n_exp // EC),
        in_specs=[pl.BlockSpec((d, tm), lambda i, c: (0, i)),
                  pl.BlockSpec((EC, d), lambda i, c: (c, 0)),
                  pl.BlockSpec((d, EC), lambda i, c: (0, c)),
                  pl.BlockSpec((PEER_HEADS, rows, tm), lambda i, c: (0, c, i)),
                  pl.BlockSpec((PEER_HEADS, rows, tm), lambda i, c: (0, c, i)),
                  pl.BlockSpec((PEER_HEADS, PEER_KEYS, tm), lambda i, c: (0, 0, i)),
                  pl.BlockSpec((PEER_HEADS, PEER_KEYS, tm), lambda i, c: (0, 0, i)),
                  pl.BlockSpec((tm, d), lambda i, c: (i, 0)),
                  pl.BlockSpec((1, 1, d), lambda i, c: (i // tps, 0, 0)),
                  pl.BlockSpec((1, d), lambda i, c: (0, 0))],
        out_specs=pl.BlockSpec((tm, d), lambda i, c: (i, 0)),
        out_shape=jax.ShapeDtypeStruct((t, d), F32),
        scratch_shapes=[pltpu.VMEM((d, tm), F32), pltpu.VMEM((EC, tm), F32), pltpu.VMEM((EC, tm), BF16)],
        compiler_params=pltpu.CompilerParams(dimension_semantics=("arbitrary", "arbitrary"),
                                             vmem_limit_bytes=VMEM_LIMIT),
        name="peer",
    )(xt, u, vt, a, n, b, rb, h1, gate2, g_post2)


def _rope_tables(seq):
    pos = jnp.arange(seq, dtype=F32)
    inv = 1.0 / (ROPE_THETA ** (jnp.arange(0, QK_ROPE, 2, dtype=F32) / QK_ROPE))
    ang = pos[:, None] * inv[None, :]
    ang = jnp.concatenate([ang, ang], axis=-1)
    cos, sin = jnp.cos(ang), jnp.sin(ang)
    pad = HEAD_PAD - QK_NOPE - QK_ROPE
    cos_t = jnp.concatenate([jnp.ones((seq, QK_NOPE), F32), cos, jnp.zeros((seq, pad), F32)], axis=1)
    sin_t = jnp.concatenate([jnp.zeros((seq, QK_NOPE), F32), sin, jnp.zeros((seq, pad), F32)], axis=1)
    q_scale = (QK_NOPE + QK_ROPE) ** -0.5 * LOG2_E
    return jnp.concatenate([cos_t, sin_t, cos_t * q_scale, sin_t * q_scale], axis=1)


def _rotate_half_cols(w):
    half = w.shape[-1] // 2
    return jnp.concatenate([-w[..., half:], w[..., :half]], axis=-1)


def _place(w, lo, width):
    pads = [(0, 0)] * (w.ndim - 1) + [(lo, width - lo - w.shape[-1])]
    return jnp.pad(w, pads)


def kernel(x, c, w_ada, b_ada, g_pre1, g_post1, w_in, w_pool_group, pool_scale, w_pool_o, g_q_lat, w_q_up,
           g_kv_lat, w_kv_up, w_mla_o, w_out, g_pre2, g_post2, w_query, sub_keys1, sub_keys2, expert_u,
           expert_v):
    batch, seq, d = x.shape
    assert w_ada.shape[0] == 1, "one layer"
    assert seq % TM_IN == 0 and seq % TQ == 0 and seq % TM_PEER == 0 and (batch * seq) % TR == 0
    t = batch * seq
    x2 = x.reshape(t, d)

    c8 = jnp.pad(c, ((0, 8 - batch), (0, 0)))
    mod = _ada(c8, w_ada[0], b_ada[0][None, :])[:batch]
    shift1, scale1, gate1, shift2, scale2, gate2 = [m[:, None, :] for m in jnp.split(mod, 6, axis=-1)]

    wi = w_in[0]
    w_kr = wi[:, 896:928]
    wa = jnp.concatenate([wi[:, :896], _place(w_kr, QK_NOPE, HEAD_PAD),
                          _place(_rotate_half_cols(w_kr), QK_NOPE, HEAD_PAD)], axis=1).astype(BF16)
    wg = wi[:, 928:].astype(BF16)
    wq3 = w_q_up[0].reshape(-1, N_HEADS, QK_NOPE + QK_ROPE)
    wq = _place(wq3, 0, HEAD_PAD).reshape(-1, N_HEADS * HEAD_PAD).astype(BF16)
    wqr = _place(_rotate_half_cols(wq3[..., QK_NOPE:]), QK_NOPE, HEAD_PAD)
    wqr = wqr.reshape(-1, N_HEADS * HEAD_PAD).astype(BF16)
    wkv3 = w_kv_up[0].reshape(-1, N_HEADS, QK_NOPE + V_HEAD)
    wk = _place(wkv3[..., :QK_NOPE], 0, HEAD_PAD).reshape(-1, N_HEADS * HEAD_PAD).astype(BF16)
    wv = wkv3[..., QK_NOPE:].reshape(-1, N_HEADS * V_HEAD).astype(BF16)

    q, k, v, p, sgb = _inproj(
        x2, scale1, shift1, g_pre1, wa, wg, w_pool_group[0].astype(BF16), pool_scale, w_pool_o[0].astype(BF16),
        g_q_lat, wq, wqr, g_kv_lat, wk, wv, _rope_tables(seq), seq)
    o = _attention(q, k, v, batch, seq)
    h1, xt = _post(o, p, sgb, x2, w_mla_o[0].astype(BF16), w_out[0].astype(BF16), g_post1, gate1, g_pre2,
                   scale2, shift2, seq)
    a, n, b, rb = _route(xt, w_query[0].T.astype(BF16), sub_keys1[0].astype(BF16), sub_keys2[0].astype(BF16))
    out = _peer(xt, expert_u[0].astype(BF16), expert_v[0].T.astype(BF16), a, n, b, rb, h1, gate2, g_post2, seq)
    return out.reshape(batch, seq, d)
```
